```python
import math
import jax
import jax.numpy as jnp
from jax import lax
import numpy as np

D_MODEL = 1024
BATCH = 1
SEQ = 16384
DEPTH = 2

GRID_W = 64
CTX_LEN = 256
RMS_EPS = 1e-6

SSD_HEADS = 6
SSD_HEAD_DIM = 64
SSD_WIDTH = SSD_HEADS * SSD_HEAD_DIM
SSD_GROUPS = 2
SSD_STATE = 128
SSD_CONV = 5
SSD_CHUNK = 128
SSD_XBC = SSD_WIDTH + 2 * SSD_GROUPS * SSD_STATE

NA_HEADS = 6
NA_HEAD_DIM = 64
NA_WIDTH = NA_HEADS * NA_HEAD_DIM
NA_ROWS = 8
NA_COLS = 16

HY_WIDTH = 256
HY_SHORT = 3
HY_EMB = 33
HY_BANDS = (HY_EMB - 1) // 2
HY_HIDDEN = 64
HY_FAST_DECAY = 0.3
HY_SLOW_DECAY = 1.5
HY_DECAY_TARGET = 1e-2

D_MIX = SSD_WIDTH + NA_WIDTH + HY_WIDTH

OFF_Z = 0
OFF_XBC = OFF_Z + SSD_WIDTH
OFF_DT = OFF_XBC + SSD_XBC
OFF_Q = OFF_DT + 2 * SSD_HEADS
OFF_K = OFF_Q + NA_WIDTH
OFF_V = OFF_K + NA_WIDTH
OFF_HY = OFF_V + NA_WIDTH
IN_COLS = OFF_HY + 3 * HY_WIDTH

N_EXPERTS = 32
TOP_K = 4
D_EXPERT = 1024
SWIGLU_LIMIT = 7.0
SWIGLU_ALPHA = 1.702
MOE_BLOCK = 128

kernel_name = "hybrid_ssd_natten_hyena_moe_dit"

F32 = jnp.float32


def rms_norm(x, g):
    xf = x.astype(F32)
    y = xf * lax.rsqrt(jnp.mean(xf * xf, axis=-1, keepdims=True) + RMS_EPS)
    return (y * g.astype(F32)).astype(x.dtype)


def dw_conv(u, w, b):
    k = w.shape[0]
    out = lax.conv_general_dilated(
        u, w[:, None, :].astype(u.dtype), window_strides=(1,), padding=[(k // 2, k // 2)],
        dimension_numbers=("NWC", "WIO", "NWC"), feature_group_count=u.shape[-1])
    return out + b.astype(u.dtype)


def ssd_scan(xs, dt, a, bm, cm, init, with_output):
    b, l, h, p = xs.shape
    g, n = bm.shape[2], bm.shape[3]
    q = SSD_CHUNK
    nc = l // q
    rep = h // g
    bh = jnp.repeat(bm.astype(F32), rep, axis=2).reshape(b, nc, q, h, n)
    xdt = (xs.astype(F32) * dt[..., None]).reshape(b, nc, q, h, p)
    a_cs = jnp.cumsum((dt * a).reshape(b, nc, q, h), axis=2)
    a_last = a_cs[:, :, -1]
    states = jnp.einsum("bcjhn,bcjh,bcjhp->bchpn", bh, jnp.exp(a_last[:, :, None] - a_cs), xdt)

    def carry_chunk(s, inp):
        st, dec = inp
        return s * dec[..., None, None] + st, s

    final, s_start = lax.scan(carry_chunk, init,
                              (jnp.moveaxis(states, 1, 0), jnp.moveaxis(jnp.exp(a_last), 1, 0)))
    if not with_output:
        return None, final
    ch = jnp.repeat(cm.astype(F32), rep, axis=2).reshape(b, nc, q, h, n)
    tri = jnp.tril(jnp.ones((q, q), dtype=bool))[:, :, None]
    seg = a_cs[:, :, :, None, :] - a_cs[:, :, None, :, :]
    decay = jnp.exp(jnp.where(tri, seg, -jnp.inf))
    scores = jnp.einsum("bcihn,bcjhn->bcijh", ch, bh) * decay
    y = (jnp.einsum("bcijh,bcjhp->bcihp", scores, xdt)
         + jnp.einsum("bcihn,bchpn,bcih->bcihp", ch, jnp.moveaxis(s_start, 0, 1), jnp.exp(a_cs)))
    return y.reshape(b, l, h, p), final


def ssd_prepare(p, conv_w, conv_b):
    b, l, _ = p.shape
    z = p[..., OFF_Z:OFF_XBC]
    xbc = jax.nn.silu(dw_conv(p[..., OFF_XBC:OFF_DT], conv_w, conv_b))
    gn = SSD_GROUPS * SSD_STATE
    xs = xbc[..., :SSD_WIDTH].reshape(b, l, SSD_HEADS, SSD_HEAD_DIM)
    bm = xbc[..., SSD_WIDTH:SSD_WIDTH + gn].reshape(b, l, SSD_GROUPS, SSD_STATE)
    cm = xbc[..., SSD_WIDTH + gn:].reshape(b, l, SSD_GROUPS, SSD_STATE)
    dt_raw = p[..., OFF_DT:OFF_Q].astype(F32).reshape(b, l, 2, SSD_HEADS)
    return z, xs, bm, cm, dt_raw


def ssd_direction(xs, bm, cm, dt_raw, a_log, dt_bias, direction, init, with_output):
    dt = jax.nn.softplus(dt_raw[:, :, direction] + dt_bias[direction].astype(F32))
    a = -jnp.exp(a_log[direction].astype(F32))
    if direction == 1:
        xs, bm, cm, dt = xs[:, ::-1], bm[:, ::-1], cm[:, ::-1], dt[:, ::-1]
    y, final = ssd_scan(xs, dt, a, bm, cm, init, with_output)
    if direction == 1 and y is not None:
        y = y[:, ::-1]
    return y, final


def ssd_output(y_sum, xs, z, d_skip, norm_g):
    b, l = xs.shape[:2]
    gsz = SSD_WIDTH // SSD_GROUPS
    y = (y_sum + d_skip.astype(F32)[:, None] * xs.astype(F32)).reshape(b, l, SSD_GROUPS, gsz)
    gated = y * jax.nn.silu(z.astype(F32)).reshape(b, l, SSD_GROUPS, gsz)
    gated = gated * lax.rsqrt(jnp.mean(gated * gated, axis=-1, keepdims=True) + RMS_EPS)
    return (gated.reshape(b, l, SSD_WIDTH) * norm_g.astype(F32)).astype(z.dtype)


def ssd_mixer(p_lat, p_ctx, conv_w, conv_b, a_log, dt_bias, d_skip, norm_g, ctx_out):
    z_c, x_c, b_c, c_c, dt_c = ssd_prepare(p_ctx, conv_w, conv_b)
    z_l, x_l, b_l, c_l, dt_l = ssd_prepare(p_lat, conv_w, conv_b)
    init = jnp.zeros((p_ctx.shape[0], SSD_HEADS, SSD_HEAD_DIM, SSD_STATE), F32)
    y_lat, y_ctx = [], []
    for direction in (0, 1):
        yc, s_ctx = ssd_direction(x_c, b_c, c_c, dt_c, a_log, dt_bias, direction, init, ctx_out)
        yl, _ = ssd_direction(x_l, b_l, c_l, dt_l, a_log, dt_bias, direction, s_ctx, True)
        y_lat.append(yl)
        y_ctx.append(yc)
    out_lat = ssd_output(y_lat[0] + y_lat[1], x_l, z_l, d_skip, norm_g)
    out_ctx = ssd_output(y_ctx[0] + y_ctx[1], x_c, z_c, d_skip, norm_g) if ctx_out else None
    return out_lat, out_ctx


def na_mixer(p_lat, p_ctx, rpb, ctx_out):
    b, l, _ = p_lat.shape
    rows = l // GRID_W
    kh = min(NA_ROWS, rows)
    scale = NA_HEAD_DIM ** -0.5

    def heads(p, off):
        return p[..., off:off + NA_WIDTH].reshape(p.shape[0], p.shape[1], NA_HEADS, NA_HEAD_DIM)

    q_grid = heads(p_lat, OFF_Q).reshape(b, rows, GRID_W, NA_HEADS, NA_HEAD_DIM)
    k_grid = heads(p_lat, OFF_K).reshape(b, rows, GRID_W, NA_HEADS, NA_HEAD_DIM)
    v_grid = heads(p_lat, OFF_V).reshape(b, rows, GRID_W, NA_HEADS, NA_HEAD_DIM)
    k_ctx, v_ctx = heads(p_ctx, OFF_K), heads(p_ctx, OFF_V)

    col = jnp.arange(GRID_W)
    col_start = jnp.clip(col - NA_COLS // 2, 0, GRID_W - NA_COLS)
    col_ok = (col[None, :] >= col_start[:, None]) & (col[None, :] < col_start[:, None] + NA_COLS)
    dc = jnp.clip(col[None, :] - col[:, None] + NA_COLS - 1, 0, 2 * NA_COLS - 2)
    bias_tab = rpb.astype(F32)

    def query_row(args):
        q_row, r = args
        r0 = jnp.clip(r - kh // 2, 0, rows - kh)
        k_blk = lax.dynamic_slice_in_dim(k_grid, r0, kh, axis=1)
        v_blk = lax.dynamic_slice_in_dim(v_grid, r0, kh, axis=1)
        dr = r0 + jnp.arange(kh) - r + NA_ROWS - 1
        bias = bias_tab[dr[None, :, None], dc[:, None, :]]
        s_loc = jnp.einsum("bqhd,bkwhd->bqkwh", q_row, k_blk, preferred_element_type=F32) * scale + bias
        s_loc = jnp.where(col_ok[None, :, None, :, None], s_loc, -jnp.inf)
        s_loc = s_loc.reshape(b, GRID_W, kh * GRID_W, NA_HEADS)
        s_ctx = jnp.einsum("bqhd,bchd->bqch", q_row, k_ctx, preferred_element_type=F32) * scale
        prob = jax.nn.softmax(jnp.concatenate([s_loc, s_ctx], axis=2), axis=2).astype(v_grid.dtype)
        p_loc = prob[:, :, :kh * GRID_W].reshape(b, GRID_W, kh, GRID_W, NA_HEADS)
        return (jnp.einsum("bqkwh,bkwhd->bqhd", p_loc, v_blk)
                + jnp.einsum("bqch,bchd->bqhd", prob[:, :, kh * GRID_W:], v_ctx))

    o = lax.map(query_row, (jnp.moveaxis(q_grid, 1, 0), jnp.arange(rows)))
    y_lat = jnp.moveaxis(o, 0, 1).reshape(b, l, NA_WIDTH)
    if not ctx_out:
        return y_lat, None
    q_ctx = heads(p_ctx, OFF_Q)
    s = jnp.einsum("bqhd,bkhd->bhqk", q_ctx, k_ctx, preferred_element_type=F32) * scale
    o_ctx = jnp.einsum("bhqk,bkhd->bqhd", jax.nn.softmax(s, axis=-1).astype(v_ctx.dtype), v_ctx)
    return y_lat, o_ctx.reshape(b, p_ctx.shape[1], NA_WIDTH)


def hyena_filters(length, w1, b1, w2, b2, w3, freq):
    t = jnp.linspace(0.0, 1.0, length, dtype=F32)[:, None]
    w = 2.0 * math.pi * jnp.arange(length, dtype=F32)[:, None] / length
    f = jnp.linspace(1e-4, HY_BANDS - 1, HY_BANDS, dtype=F32)[None, :]
    z = jnp.concatenate([t, jnp.cos(f * w), -jnp.sin(f * w)], axis=1)
    freq = freq.astype(F32)
    h = jnp.sin(freq[0] * (z @ w1.astype(F32) + b1.astype(F32)))
    h = jnp.sin(freq[1] * (h @ w2.astype(F32) + b2.astype(F32)))
    h = (h @ w3.astype(F32)).reshape(length, 2, HY_WIDTH)
    max_decay = math.log(HY_DECAY_TARGET) / HY_FAST_DECAY
    min_decay = math.log(HY_DECAY_TARGET) / HY_SLOW_DECAY
    deltas = jnp.abs(jnp.linspace(min_decay, max_decay, HY_WIDTH, dtype=F32))
    h = h * jnp.exp(-t[:, :, None] * deltas)
    h = h / jnp.sum(jnp.abs(h), axis=(0, 1), keepdims=True)
    return h[:, 0], h[:, 1]


def long_conv(u, h_fwd, h_bwd, d_bias):
    b, l, ch = u.shape
    k = jnp.concatenate([h_fwd, jnp.zeros((1, ch), F32), h_bwd[:0:-1]], axis=0)
    uf = u.astype(F32)
    y = jnp.fft.irfft(jnp.fft.rfft(uf, n=2 * l, axis=1) * jnp.fft.rfft(k, axis=0)[None], n=2 * l, axis=1)[:, :l]
    return (y + uf * d_bias.astype(F32)).astype(u.dtype)


def hyena_mixer(p, conv_w, conv_b, f_w1, f_b1, f_w2, f_b2, f_w3, freq, d_bias):
    uc = dw_conv(p[..., OFF_HY:IN_COLS], conv_w, conv_b)
    x0, x1, v = jnp.split(uc, 3, axis=-1)
    h_fwd, h_bwd = hyena_filters(p.shape[1], f_w1, f_b1, f_w2, f_b2, f_w3, freq)
    return x0 * long_conv(v * x1, h_fwd, h_bwd, d_bias)


def moe_ffn(h, w_router, b_router, w_gate_up, b_gate_up, w_down, b_down):
    t, d = h.shape
    logits = (h @ w_router + b_router).astype(F32)
    top_logit, top_idx = lax.top_k(logits, TOP_K)
    gate_w = jax.nn.softmax(top_logit, axis=-1)
    n = t * TOP_K
    flat_e = top_idx.reshape(n)
    order = jnp.argsort(flat_e)
    sorted_e = flat_e[order]
    counts = jnp.bincount(flat_e, length=N_EXPERTS)
    padded = (counts + MOE_BLOCK - 1) // MOE_BLOCK * MOE_BLOCK
    pad_end = jnp.cumsum(padded)
    pad_start = pad_end - padded
    start = jnp.cumsum(counts) - counts
    dest = pad_start[sorted_e] + jnp.arange(n) - start[sorted_e]
    n_blocks = (n + N_EXPERTS * (MOE_BLOCK - 1) + MOE_BLOCK - 1) // MOE_BLOCK
    n_rows = n_blocks * MOE_BLOCK
    row_tok = jnp.full((n_rows,), t, jnp.int32).at[dest].set((order // TOP_K).astype(jnp.int32))
    row_w = jnp.zeros((n_rows,), F32).at[dest].set(gate_w.reshape(n)[order])
    block_e = jnp.minimum(jnp.searchsorted(pad_end, jnp.arange(n_blocks) * MOE_BLOCK, side="right"),
                          N_EXPERTS - 1)
    h_pad = jnp.concatenate([h, jnp.zeros((1, d), h.dtype)], axis=0)
    xb = h_pad[row_tok].reshape(n_blocks, MOE_BLOCK, d)

    def expert_block(args):
        xblk, e = args
        gu = xblk @ w_gate_up[e] + b_gate_up[e]
        gate, up = jnp.split(gu, 2, axis=-1)
        gate = jnp.minimum(gate, SWIGLU_LIMIT)
        up = jnp.clip(up, -SWIGLU_LIMIT, SWIGLU_LIMIT)
        act = gate * jax.nn.sigmoid(SWIGLU_ALPHA * gate) * (up + 1.0)
        return act @ w_down[e] + b_down[e]

    yb = lax.map(expert_block, (xb, block_e)).reshape(n_rows, d)
    y = jax.ops.segment_sum(yb * row_w[:, None].astype(yb.dtype), row_tok, num_segments=t + 1)
    return y[:t]


def trunk_layer(x, ctx, c, c_ctx, lp, last):
    b, l, d = x.shape
    mod = jax.nn.silu(c) @ lp["w_mod"] + lp["b_mod"]
    mod_c = jax.nn.silu(c_ctx) @ lp["w_mod"] + lp["b_mod"]
    sh1, sc1, g1, sh2, sc2, g2 = jnp.split(mod[:, None, :], 6, axis=-1)
    sh1c, sc1c, g1c, sh2c, sc2c, g2c = jnp.split(mod_c, 6)
    gains = lp["norm_g"]

    h = rms_norm(x, gains[0]) * (1.0 + sc1) + sh1
    hc = rms_norm(ctx, gains[0]) * (1.0 + sc1c) + sh1c
    p = h @ lp["w_in"]
    pc = hc @ lp["w_in"]
    ctx_out = not last
    y_ssd, y_ssd_c = ssd_mixer(p, pc, lp["ssd_conv_w"], lp["ssd_conv_b"], lp["ssd_a_log"],
                               lp["ssd_dt_bias"], lp["ssd_d"], lp["ssd_norm_g"], ctx_out)
    y_na, y_na_c = na_mixer(p, pc, lp["na_rpb"], ctx_out)
    hy_args = (lp["hy_conv_w"], lp["hy_conv_b"], lp["hy_f_w1"], lp["hy_f_b1"], lp["hy_f_w2"],
               lp["hy_f_b2"], lp["hy_f_w3"], lp["hy_freq"], lp["hy_d"])
    y_hy = hyena_mixer(p, *hy_args)
    mix = jnp.concatenate([y_ssd, y_na, y_hy], axis=-1) @ lp["w_out"]
    x = x + g1 * rms_norm(mix, gains[1])
    h2 = rms_norm(x, gains[2]) * (1.0 + sc2) + sh2
    moe_args = (lp["w_router"], lp["b_router"], lp["w_gate_up"], lp["b_gate_up"], lp["w_down"], lp["b_down"])

    if last:
        f = moe_ffn(h2.reshape(b * l, d), *moe_args).reshape(b, l, d)
        return x + g2 * rms_norm(f, gains[3]), None

    y_hy_c = hyena_mixer(pc, *hy_args)
    mix_c = jnp.concatenate([y_ssd_c, y_na_c, y_hy_c], axis=-1) @ lp["w_out"]
    ctx = ctx + g1c * rms_norm(mix_c, gains[1])
    h2c = rms_norm(ctx, gains[2]) * (1.0 + sc2c) + sh2c
    lc = ctx.shape[1]
    f = moe_ffn(jnp.concatenate([h2.reshape(b * l, d), h2c.reshape(b * lc, d)], axis=0), *moe_args)
    x = x + g2 * rms_norm(f[:b * l].reshape(b, l, d), gains[3])
    ctx = ctx + g2c * rms_norm(f[b * l:].reshape(b, lc, d), gains[3])
    return x, ctx


def setup_inputs(seed: int = 0) -> dict:
    key = jax.random.key(seed)
    ks = jax.random.split(key, 32)

    def nrm(i, shape, scale):
        return jax.random.normal(ks[i], shape, F32) * scale

    x = nrm(0, (BATCH, SEQ, D_MODEL), 1.0)
    c = nrm(1, (BATCH, D_MODEL), 1.0)
    ctx = nrm(2, (BATCH, CTX_LEN, D_MODEL), 1.0)
    c_ctx = nrm(3, (D_MODEL,), 1.0)
    w_mod = nrm(4, (DEPTH, D_MODEL, 6 * D_MODEL), 0.5 * D_MODEL ** -0.5)
    b_mod = nrm(5, (DEPTH, 6 * D_MODEL), 0.02)
    norm_g = 1.0 + nrm(6, (DEPTH, 4, D_MODEL), 0.05)
    w_in = nrm(7, (DEPTH, D_MODEL, IN_COLS), D_MODEL ** -0.5)
    ssd_conv_w = nrm(8, (DEPTH, SSD_CONV, SSD_XBC), SSD_CONV ** -0.5)
    ssd_conv_b = nrm(9, (DEPTH, SSD_XBC), 0.01)
    ssd_a_log = jnp.log(jax.random.uniform(ks[10], (DEPTH, 2, SSD_HEADS), F32, 1.0, 16.0))
    dt0 = jnp.exp(jax.random.uniform(ks[11], (DEPTH, 2, SSD_HEADS), F32, math.log(1e-3), math.log(1e-1)))
    ssd_dt_bias = dt0 + jnp.log(-jnp.expm1(-dt0))
    ssd_d = 1.0 + nrm(12, (DEPTH, SSD_HEADS), 0.1)
    ssd_norm_g = 1.0 + nrm(13, (DEPTH, SSD_WIDTH), 0.05)
    na_rpb = nrm(14, (DEPTH, 2 * NA_ROWS - 1, 2 * NA_COLS - 1, NA_HEADS), 0.1)
    hy_conv_w = nrm(15, (DEPTH, HY_SHORT, 3 * HY_WIDTH), HY_SHORT ** -0.5)
    hy_conv_b = nrm(16, (DEPTH, 3 * HY_WIDTH), 0.01)
    hy_f_w1 = nrm(17, (DEPTH, HY_EMB, HY_HIDDEN), HY_EMB ** -0.5)
    hy_f_b1 = nrm(18, (DEPTH, HY_HIDDEN), 0.1)
    hy_f_w2 = nrm(19, (DEPTH, HY_HIDDEN, HY_HIDDEN), HY_HIDDEN ** -0.5)
    hy_f_b2 = nrm(20, (DEPTH, HY_HIDDEN), 0.1)
    hy_f_w3 = nrm(21, (DEPTH, HY_HIDDEN, 2 * HY_WIDTH), HY_HIDDEN ** -0.5)
    hy_freq = 1.0 + nrm(22, (DEPTH, 2, HY_HIDDEN), 0.1)
    hy_d = nrm(23, (DEPTH, HY_WIDTH), 0.5)
    w_out = nrm(24, (DEPTH, D_MIX, D_MODEL), D_MIX ** -0.5)
    w_router = nrm(25, (DEPTH, D_MODEL, N_EXPERTS), D_MODEL ** -0.5)
    b_router = nrm(26, (DEPTH, N_EXPERTS), 0.01)
    w_gate_up = nrm(27, (DEPTH, N_EXPERTS, D_MODEL, 2 * D_EXPERT), D_MODEL ** -0.5)
    b_gate_up = nrm(28, (DEPTH, N_EXPERTS, 2 * D_EXPERT), 0.01)
    w_down = nrm(29, (DEPTH, N_EXPERTS, D_EXPERT, D_MODEL), D_EXPERT ** -0.5)
    b_down = nrm(30, (DEPTH, N_EXPERTS, D_MODEL), 0.01)
    return {"x": x, "c": c, "ctx": ctx, "c_ctx": c_ctx, "w_mod": w_mod, "b_mod": b_mod,
            "norm_g": norm_g, "w_in": w_in, "ssd_conv_w": ssd_conv_w, "ssd_conv_b": ssd_conv_b,
            "ssd_a_log": ssd_a_log, "ssd_dt_bias": ssd_dt_bias, "ssd_d": ssd_d, "ssd_norm_g": ssd_norm_g,
            "na_rpb": na_rpb, "hy_conv_w": hy_conv_w, "hy_conv_b": hy_conv_b, "hy_f_w1": hy_f_w1,
            "hy_f_b1": hy_f_b1, "hy_f_w2": hy_f_w2, "hy_f_b2": hy_f_b2, "hy_f_w3": hy_f_w3,
            "hy_freq": hy_freq, "hy_d": hy_d, "w_out": w_out, "w_router": w_router, "b_router": b_router,
            "w_gate_up": w_gate_up, "b_gate_up": b_gate_up, "w_down": w_down, "b_down": b_down}


def reference(x, c, ctx, c_ctx, w_mod, b_mod, norm_g, w_in, ssd_conv_w, ssd_conv_b, ssd_a_log,
              ssd_dt_bias, ssd_d, ssd_norm_g, na_rpb, hy_conv_w, hy_conv_b, hy_f_w1, hy_f_b1,
              hy_f_w2, hy_f_b2, hy_f_w3, hy_freq, hy_d, w_out, w_router, b_router, w_gate_up,
              b_gate_up, w_down, b_down):
    for i in range(DEPTH):
        lp = {"w_mod": w_mod[i], "b_mod": b_mod[i], "norm_g": norm_g[i], "w_in": w_in[i],
              "ssd_conv_w": ssd_conv_w[i], "ssd_conv_b": ssd_conv_b[i], "ssd_a_log": ssd_a_log[i],
              "ssd_dt_bias": ssd_dt_bias[i], "ssd_d": ssd_d[i], "ssd_norm_g": ssd_norm_g[i],
              "na_rpb": na_rpb[i], "hy_conv_w": hy_conv_w[i], "hy_conv_b": hy_conv_b[i],
              "hy_f_w1": hy_f_w1[i], "hy_f_b1": hy_f_b1[i], "hy_f_w2": hy_f_w2[i], "hy_f_b2": hy_f_b2[i],
              "hy_f_w3": hy_f_w3[i], "hy_freq": hy_freq[i], "hy_d": hy_d[i], "w_out": w_out[i],
              "w_router": w_router[i], "b_router": b_router[i], "w_gate_up": w_gate_up[i],
              "b_gate_up": b_gate_up[i], "w_down": w_down[i], "b_down": b_down[i]}
        x, ctx = trunk_layer(x, ctx, c, c_ctx, lp, i == DEPTH - 1)
    return x
```

```python
import functools
import math

import jax
import jax.numpy as jnp
import numpy as np
from jax import lax
from jax.experimental import pallas as pl
from jax.experimental.pallas import tpu as pltpu

F32 = jnp.float32
BF16 = jnp.bfloat16

D_MODEL = 1024
GRID_W = 64
RMS_EPS = 1e-6

SSD_HEADS = 6
SSD_HEAD_DIM = 64
SSD_WIDTH = SSD_HEADS * SSD_HEAD_DIM
SSD_GROUPS = 2
SSD_STATE = 128
SSD_CONV = 5
SSD_CHUNK = 128
SSD_XBC = SSD_WIDTH + 2 * SSD_GROUPS * SSD_STATE

NA_HEADS = 6
NA_HEAD_DIM = 64
NA_WIDTH = NA_HEADS * NA_HEAD_DIM
NA_ROWS = 8
NA_COLS = 16

HY_WIDTH = 256
HY_SHORT = 3
HY_EMB = 33
HY_BANDS = (HY_EMB - 1) // 2
HY_HIDDEN = 64
HY_FAST_DECAY = 0.3
HY_SLOW_DECAY = 1.5
HY_DECAY_TARGET = 1e-2

OFF_Z = 0
OFF_XBC = OFF_Z + SSD_WIDTH
OFF_DT = OFF_XBC + SSD_XBC
OFF_Q = OFF_DT + 2 * SSD_HEADS
OFF_K = OFF_Q + NA_WIDTH
OFF_V = OFF_K + NA_WIDTH
OFF_HY = OFF_V + NA_WIDTH
IN_COLS = OFF_HY + 3 * HY_WIDTH

N_EXPERTS = 32
TOP_K = 4
D_EXPERT = 1024
SWIGLU_LIMIT = 7.0
SWIGLU_ALPHA = 1.702

LANE = 128
SUBLANE = 8
TM = 256
DT_PAD = 2 * LANE
VMEM_LIMIT = 56 * 1024 * 1024
NEG_BIG = -1e30


def _cparams(*sem):
    return pltpu.CompilerParams(dimension_semantics=sem, vmem_limit_bytes=VMEM_LIMIT)


def _silu(x):
    return x * jax.nn.sigmoid(x)


def _softplus(x):
    return jnp.maximum(x, 0.0) + jnp.log1p(jnp.exp(-jnp.abs(x)))


def _full(shape):
    return pl.BlockSpec(shape, lambda *_: (0,) * len(shape))


def _mod_kernel(cc_ref, w_ref, b_ref, o_ref):
    a = _silu(cc_ref[...]).astype(BF16)
    o_ref[...] = jnp.dot(a, w_ref[...].astype(BF16), preferred_element_type=F32) + b_ref[...]


def adaln_mod(cc, w_mod, b_mod, layer):
    n = w_mod.shape[2]
    tn = 1536
    return pl.pallas_call(
        _mod_kernel,
        grid=(n // tn,),
        in_specs=[_full((SUBLANE, D_MODEL)),
                  pl.BlockSpec((None, D_MODEL, tn), lambda j: (layer, 0, j)),
                  pl.BlockSpec((None, 1, tn), lambda j: (layer, 0, j))],
        out_specs=pl.BlockSpec((SUBLANE, tn), lambda j: (0, j)),
        out_shape=jax.ShapeDtypeStruct((SUBLANE, n), F32),
        compiler_params=_cparams("arbitrary"),
        name="adaln_mod",
    )(cc, w_mod, b_mod.reshape(b_mod.shape[0], 1, n))


def _modulated_norm(x, gain, mod_ref, which, is_ctx):
    d = D_MODEL
    y = x * lax.rsqrt(jnp.mean(x * x, axis=-1, keepdims=True) + RMS_EPS) * gain
    sh = jnp.where(is_ctx, mod_ref[1:2, which * 3 * d: which * 3 * d + d], mod_ref[0:1, which * 3 * d: which * 3 * d + d])
    sc = jnp.where(is_ctx, mod_ref[1:2, which * 3 * d + d: which * 3 * d + 2 * d],
                   mod_ref[0:1, which * 3 * d + d: which * 3 * d + 2 * d])
    return y * (1.0 + sc) + sh


IN_SPLITS = (SSD_WIDTH, SSD_XBC, DT_PAD, NA_WIDTH, NA_WIDTH, NA_WIDTH, 3 * HY_WIDTH)
IN_PAD_COLS = sum(IN_SPLITS)


def _inproj_kernel(x_ref, mod_ref, g_ref, w_ref, z_ref, xbc_ref, dt_ref, q_ref, k_ref, v_ref, hy_ref):
    is_ctx = pl.program_id(0) == 0
    h = _modulated_norm(x_ref[...], g_ref[...], mod_ref, 0, is_ctx).astype(BF16)
    outs = (z_ref, xbc_ref, dt_ref, q_ref, k_ref, v_ref, hy_ref)
    off = 0
    for o_ref, wdt in zip(outs, IN_SPLITS):
        o_ref[...] = jnp.dot(h, w_ref[:, off:off + wdt], preferred_element_type=F32).astype(o_ref.dtype)
        off += wdt


def in_projection(xt, mod, gain, w_pad):
    t = xt.shape[0]
    dts = (F32, F32, F32, BF16, BF16, BF16, F32)
    return pl.pallas_call(
        _inproj_kernel,
        grid=(t // TM,),
        in_specs=[pl.BlockSpec((TM, D_MODEL), lambda i: (i, 0)),
                  _full((SUBLANE, 6 * D_MODEL)),
                  _full((1, D_MODEL)),
                  _full((D_MODEL, IN_PAD_COLS))],
        out_specs=[pl.BlockSpec((TM, w), lambda i: (i, 0)) for w in IN_SPLITS],
        out_shape=[jax.ShapeDtypeStruct((t, w), dt) for w, dt in zip(IN_SPLITS, dts)],
        compiler_params=_cparams("arbitrary"),
        name="in_projection",
    )(xt, mod, gain, w_pad)


def _pad_dirs(a):
    pad = [(0, 0)] * (a.ndim - 1) + [(0, LANE - SSD_HEADS)]
    return jnp.concatenate([jnp.pad(a[..., :SSD_HEADS], pad), jnp.pad(a[..., SSD_HEADS:], pad)], axis=-1)


def pack_w_in(w_in):
    parts = [w_in[:, OFF_Z:OFF_XBC], w_in[:, OFF_XBC:OFF_DT],
             _pad_dirs(w_in[:, OFF_DT:OFF_Q]),
             w_in[:, OFF_Q:OFF_K], w_in[:, OFF_K:OFF_V], w_in[:, OFF_V:OFF_HY], w_in[:, OFF_HY:IN_COLS]]
    return jnp.concatenate(parts, axis=1).astype(BF16)


HALO = SUBLANE


def _shifted_taps(prev_ref, cur_ref, next_ref, w_ref, taps, first, last):
    prev = jnp.where(first, 0.0, prev_ref[...])
    nxt = jnp.where(last, 0.0, next_ref[...])
    ext = jnp.concatenate([prev, cur_ref[...], nxt], axis=0)
    half = taps // 2
    acc = None
    for j in range(taps):
        term = ext[HALO - half + j: HALO - half + j + TM, :] * w_ref[j:j + 1, :]
        acc = term if acc is None else acc + term
    return acc


def _pre_kernel(xp_ref, xc_ref, xn_ref, hp_ref, hc_ref, hn_ref, dt_ref,
                cw_ref, cb_ref, hw_ref, hb_ref, dtb_ref,
                xbc_ref, dto_ref, x0_ref, u_ref):
    i = pl.program_id(0)
    n = pl.num_programs(0)
    first = jnp.logical_or(i == 0, i == 1)
    last = jnp.logical_or(i == 0, i == n - 1)
    xbc = _shifted_taps(xp_ref, xc_ref, xn_ref, cw_ref, SSD_CONV, first, last) + cb_ref[...]
    xbc_ref[...] = _silu(xbc)
    uc = _shifted_taps(hp_ref, hc_ref, hn_ref, hw_ref, HY_SHORT, first, last) + hb_ref[...]
    x0_ref[...] = uc[:, :HY_WIDTH]
    u_ref[...] = uc[:, 2 * HY_WIDTH:] * uc[:, HY_WIDTH:2 * HY_WIDTH]
    dto_ref[...] = _softplus(dt_ref[...] + dtb_ref[...])


def pre_mixers(p_xbc, p_hy, p_dt, conv_w, conv_b, hy_w, hy_b, dt_bias):
    t = p_xbc.shape[0]
    nt = t // TM
    r = TM // HALO
    nh = t // HALO

    def cur(w):
        return pl.BlockSpec((TM, w), lambda i: (i, 0))

    def prev(w):
        return pl.BlockSpec((HALO, w), lambda i: (jnp.maximum(i * r - 1, 0), 0))

    def nxt(w):
        return pl.BlockSpec((HALO, w), lambda i: (jnp.minimum((i + 1) * r, nh - 1), 0))

    cw = jnp.pad(conv_w, ((0, SUBLANE - SSD_CONV), (0, 0)))
    hw = jnp.pad(hy_w, ((0, SUBLANE - HY_SHORT), (0, 0)))
    dtb = _pad_dirs(dt_bias.reshape(1, -1))
    hyw = 3 * HY_WIDTH
    return pl.pallas_call(
        _pre_kernel,
        grid=(nt,),
        in_specs=[prev(SSD_XBC), cur(SSD_XBC), nxt(SSD_XBC), prev(hyw), cur(hyw), nxt(hyw), cur(DT_PAD),
                  _full((SUBLANE, SSD_XBC)), _full((1, SSD_XBC)), _full((SUBLANE, hyw)), _full((1, hyw)),
                  _full((1, DT_PAD))],
        out_specs=[cur(SSD_XBC), cur(DT_PAD), cur(HY_WIDTH), cur(HY_WIDTH)],
        out_shape=[jax.ShapeDtypeStruct((t, SSD_XBC), F32), jax.ShapeDtypeStruct((t, DT_PAD), F32),
                   jax.ShapeDtypeStruct((t, HY_WIDTH), F32), jax.ShapeDtypeStruct((t, HY_WIDTH), F32)],
        compiler_params=_cparams("arbitrary"),
        name="pre_mixers",
    )(p_xbc, p_xbc, p_xbc, p_hy, p_hy, p_hy, p_dt, cw, conv_b.reshape(1, -1), hw, hy_b.reshape(1, -1), dtb)


def _split3(v):
    hi = v.astype(BF16)
    r1 = v - hi.astype(F32)
    mid = r1.astype(BF16)
    lo = (r1 - mid.astype(F32)).astype(BF16)
    return hi, mid, lo


def _ssd_kernel(xbc_ref, dt_ref, a_ref, y_ref, st_ref):
    d = pl.program_id(0)
    c = pl.program_id(1)
    q = SSD_CHUNK
    p = SSD_HEAD_DIM
    hpg = SSD_HEADS // SSD_GROUPS

    @pl.when(c == 0)
    def _():
        st_ref[...] = jnp.zeros_like(st_ref)

    row = lax.broadcasted_iota(jnp.int32, (q, q), 0)
    col = lax.broadcasted_iota(jnp.int32, (q, q), 1)
    fwd = d == 0
    mask = jnp.where(fwd, col - row, row - col) <= 0
    tri = mask.astype(BF16)

    xbc = xbc_ref[...]
    xs = xbc[:, :SSD_WIDTH]
    bm = xbc[:, SSD_WIDTH:SSD_WIDTH + SSD_GROUPS * SSD_STATE].astype(BF16)
    cm = xbc[:, SSD_WIDTH + SSD_GROUPS * SSD_STATE:].astype(BF16)
    dt = dt_ref[...]
    da = dt * a_ref[...]
    cs = sum(jnp.dot(tri, part, preferred_element_type=F32) for part in _split3(da))
    cs_t = cs.T
    total = jnp.where(fwd, cs[q - 1:q, :], cs[0:1, :])
    e_in = jnp.exp(cs)
    e_out = jnp.exp(total - cs)
    e_tot = jnp.exp(total)

    ys = []
    for g in range(SSD_GROUPS):
        bg = bm[:, g * SSD_STATE:(g + 1) * SSD_STATE]
        cg = cm[:, g * SSD_STATE:(g + 1) * SSD_STATE]
        scores = lax.dot_general(cg, bg, (((1,), (1,)), ((), ())), preferred_element_type=F32)
        st = st_ref[g]
        y_state = jnp.dot(cg, st.astype(BF16), preferred_element_type=F32)
        new_st = []
        for hh in range(hpg):
            h = g * hpg + hh
            seg = cs[:, h:h + 1] - cs_t[h:h + 1, :]
            decay = jnp.exp(jnp.where(mask, seg, NEG_BIG))
            xdt = xs[:, h * p:(h + 1) * p] * dt[:, h:h + 1]
            y = jnp.dot((scores * decay).astype(BF16), xdt.astype(BF16), preferred_element_type=F32)
            ys.append(y + y_state[:, hh * p:(hh + 1) * p] * e_in[:, h:h + 1])
            xw = (xdt * e_out[:, h:h + 1]).astype(BF16)
            upd = lax.dot_general(bg, xw, (((0,), (0,)), ((), ())), preferred_element_type=F32)
            new_st.append(st[:, hh * p:(hh + 1) * p] * e_tot[:, h:h + 1] + upd)
        st_ref[g] = jnp.concatenate(new_st, axis=1)
    y_ref[...] = jnp.concatenate(ys, axis=1)


def ssd_scan(xbc_act, dt, a_log, n_ctx):
    t = xbc_act.shape[0]
    nct = t // SSD_CHUNK
    ncx = n_ctx // SSD_CHUNK

    def chunk(d, c):
        back = jnp.where(c < ncx, ncx - 1 - c, nct - 1 + ncx - c)
        return jnp.where(d == 0, c, back)

    a = _pad_dirs(-jnp.exp(a_log.astype(F32)).reshape(1, -1))
    hpg = SSD_HEADS // SSD_GROUPS
    return pl.pallas_call(
        _ssd_kernel,
        grid=(2, nct),
        in_specs=[pl.BlockSpec((SSD_CHUNK, SSD_XBC), lambda d, c: (chunk(d, c), 0)),
                  pl.BlockSpec((SSD_CHUNK, LANE), lambda d, c: (chunk(d, c), d)),
                  pl.BlockSpec((1, LANE), lambda d, c: (0, d))],
        out_specs=pl.BlockSpec((None, SSD_CHUNK, SSD_WIDTH), lambda d, c: (d, chunk(d, c), 0)),
        out_shape=jax.ShapeDtypeStruct((2, t, SSD_WIDTH), F32),
        scratch_shapes=[pltpu.VMEM((SSD_GROUPS, SSD_STATE, hpg * SSD_HEAD_DIM), F32)],
        compiler_params=_cparams("arbitrary", "arbitrary"),
        name="ssd_scan",
    )(xbc_act, dt, a)


NA_PAIRS = NA_HEADS // 2
NA_LOCAL = NA_ROWS * GRID_W
NA_VARIANTS = NA_ROWS + 1


def na_bias_tiles(rpb):
    qc = jnp.arange(GRID_W)
    kc = jnp.arange(GRID_W)
    col_start = jnp.clip(qc - NA_COLS // 2, 0, GRID_W - NA_COLS)
    ok = (kc[None, :] >= col_start[:, None]) & (kc[None, :] < col_start[:, None] + NA_COLS)
    dc = jnp.clip(kc[None, :] - qc[:, None] + NA_COLS - 1, 0, 2 * NA_COLS - 2)
    ncol = 2 * NA_COLS - 1
    sel = (dc[:, :, None] == jnp.arange(ncol)[None, None, :]).astype(F32)
    by_dr = jnp.einsum("qkc,rch->rhqk", sel, rpb.astype(F32), precision=lax.Precision.HIGHEST)
    by_dr = jnp.where(ok[None, None, :, :], by_dr, NEG_BIG)
    tab = jnp.stack([by_dr[NA_ROWS - 1 - o: 2 * NA_ROWS - 1 - o] for o in range(NA_ROWS)])
    tab = jnp.transpose(tab, (0, 2, 3, 1, 4)).reshape(NA_ROWS, NA_PAIRS, 2 * GRID_W, NA_LOCAL)
    none = jnp.full((1, NA_PAIRS, 2 * GRID_W, NA_LOCAL), NEG_BIG, F32)
    return jnp.concatenate([tab, none], axis=0)


def _na_kernel(q_ref, kl_ref, vl_ref, kc_ref, vc_ref, bias_ref, o_ref):
    w = GRID_W
    n_ctx = kc_ref.shape[0] * w
    q = q_ref[0]
    kl = kl_ref[...].reshape(NA_LOCAL, NA_WIDTH)
    vl = vl_ref[...].reshape(NA_LOCAL, NA_WIDTH)
    kc = kc_ref[...].reshape(n_ctx, NA_WIDTH)
    vc = vc_ref[...].reshape(n_ctx, NA_WIDTH)
    lane = lax.broadcasted_iota(jnp.int32, (w, LANE), 1)
    first = lane < NA_HEAD_DIM
    nt = (((1,), (1,)), ((), ()))
    outs = []
    for pr in range(NA_PAIRS):
        sl = slice(pr * LANE, (pr + 1) * LANE)
        qp = q[:, sl] * (NA_HEAD_DIM ** -0.5)
        zero = jnp.zeros_like(qp)
        qbd = jnp.concatenate([jnp.where(first, qp, zero), jnp.where(first, zero, qp)], axis=0)
        s_loc = lax.dot_general(qbd, kl[:, sl], nt, preferred_element_type=F32) + bias_ref[pr]
        s_ctx = lax.dot_general(qbd, kc[:, sl], nt, preferred_element_type=F32)
        m = jnp.maximum(jnp.max(s_loc, axis=-1, keepdims=True), jnp.max(s_ctx, axis=-1, keepdims=True))
        p_loc = jnp.exp(s_loc - m)
        p_ctx = jnp.exp(s_ctx - m)
        denom = jnp.sum(p_loc, axis=-1, keepdims=True) + jnp.sum(p_ctx, axis=-1, keepdims=True)
        o = (jnp.dot(p_loc.astype(BF16), vl[:, sl], preferred_element_type=F32)
             + jnp.dot(p_ctx.astype(BF16), vc[:, sl], preferred_element_type=F32)) / denom
        outs.append(jnp.where(first, o[:w], o[w:]))
    o_ref[0] = jnp.concatenate(outs, axis=1)


def na_attention(p_q, p_k, p_v, bias_tiles, n_ctx):
    t = p_q.shape[0]
    w = GRID_W
    nb = t // w
    cb = n_ctx // w
    rows = nb - cb
    q3, k3, v3 = (a.reshape(nb, w, NA_WIDTH) for a in (p_q, p_k, p_v))

    def r0(s):
        return jnp.clip(s - cb - NA_ROWS // 2, 0, rows - NA_ROWS)

    def variant(s):
        return jnp.where(s < cb, NA_ROWS, s - cb - r0(s))

    win = pl.BlockSpec((pl.Element(NA_ROWS), pl.Element(w), pl.Element(NA_WIDTH)), lambda s: (r0(s) + cb, 0, 0))
    ctx = pl.BlockSpec((cb, w, NA_WIDTH), lambda s: (0, 0, 0))
    out = pl.pallas_call(
        _na_kernel,
        grid=(nb,),
        in_specs=[pl.BlockSpec((1, w, NA_WIDTH), lambda s: (s, 0, 0)), win, win, ctx, ctx,
                  pl.BlockSpec((None, NA_PAIRS, 2 * w, NA_LOCAL), lambda s: (variant(s), 0, 0, 0))],
        out_specs=pl.BlockSpec((1, w, NA_WIDTH), lambda s: (s, 0, 0)),
        out_shape=jax.ShapeDtypeStruct((nb, w, NA_WIDTH), F32),
        compiler_params=_cparams("arbitrary"),
        name="na_attention",
    )(q3, k3, v3, k3, v3, bias_tiles)
    return out.reshape(t, NA_WIDTH)


HY_ROWS = 512


def _hy_filter_kernel(fl_ref, w1_ref, b1_ref, w2_ref, b2_ref, w3_ref, fr_ref, dl_ref, k_ref, nrm_ref, *, length):
    i = pl.program_id(0)
    rows = HY_ROWS
    r = lax.broadcasted_iota(jnp.int32, (rows, 1), 0) + i * rows
    is_fwd = r < length
    pos = jnp.where(is_fwd, r, jnp.where(r == length, 0, 2 * length - r)).astype(F32)
    t = pos / float(length - 1)
    w = (2.0 * math.pi) * pos / float(length)
    lane = lax.broadcasted_iota(jnp.int32, (rows, LANE), 1)
    arg = fl_ref[...] * w
    z = jnp.where(lane == 0, t, jnp.where(lane <= HY_BANDS, jnp.cos(arg),
                                          jnp.where(lane <= 2 * HY_BANDS, -jnp.sin(arg), 0.0)))
    h = jnp.dot(z.astype(BF16), w1_ref[...], preferred_element_type=F32) + b1_ref[...]
    h = jnp.sin(fr_ref[0:1, :] * h)
    h = jnp.dot(h.astype(BF16), w2_ref[...], preferred_element_type=F32) + b2_ref[...]
    h = jnp.sin(fr_ref[1:2, :] * h)
    h = jnp.dot(h.astype(BF16), w3_ref[...], preferred_element_type=F32)
    tap = jnp.where(is_fwd, h[:, :HY_WIDTH], h[:, HY_WIDTH:]) * jnp.exp(-t * dl_ref[...])
    k_ref[...] = jnp.where(r == length, 0.0, tap)

    @pl.when(i == 0)
    def _():
        nrm_ref[...] = jnp.zeros_like(nrm_ref)

    nrm_ref[...] += jnp.sum(jnp.abs(tap), axis=0, keepdims=True)


def hyena_filter(length, w1, b1, w2, b2, w3, freq):
    f = jnp.linspace(1e-4, HY_BANDS - 1, HY_BANDS, dtype=F32)
    bands = jnp.concatenate([jnp.zeros((1,), F32), f, f, jnp.zeros((LANE - 1 - 2 * HY_BANDS,), F32)]).reshape(1, LANE)
    max_decay = math.log(HY_DECAY_TARGET) / HY_FAST_DECAY
    min_decay = math.log(HY_DECAY_TARGET) / HY_SLOW_DECAY
    deltas = jnp.abs(jnp.linspace(min_decay, max_decay, HY_WIDTH, dtype=F32)).reshape(1, HY_WIDTH)
    hp = LANE - HY_HIDDEN
    w1p = jnp.pad(w1, ((0, LANE - HY_EMB), (0, hp))).astype(BF16)
    w2p = jnp.pad(w2, ((0, hp), (0, hp))).astype(BF16)
    w3p = jnp.pad(w3, ((0, hp), (0, 0))).astype(BF16)
    b1p = jnp.pad(b1.reshape(1, -1), ((0, 0), (0, hp)))
    b2p = jnp.pad(b2.reshape(1, -1), ((0, 0), (0, hp)))
    frp = jnp.pad(freq, ((0, SUBLANE - 2), (0, hp)))
    n = 2 * length
    return pl.pallas_call(
        functools.partial(_hy_filter_kernel, length=length),
        grid=(n // HY_ROWS,),
        in_specs=[_full((1, LANE)), _full((LANE, LANE)), _full((1, LANE)), _full((LANE, LANE)), _full((1, LANE)),
                  _full((LANE, 2 * HY_WIDTH)), _full((SUBLANE, LANE)), _full((1, HY_WIDTH))],
        out_specs=[pl.BlockSpec((HY_ROWS, HY_WIDTH), lambda i: (i, 0)), _full((1, HY_WIDTH))],
        out_shape=[jax.ShapeDtypeStruct((n, HY_WIDTH), F32), jax.ShapeDtypeStruct((1, HY_WIDTH), F32)],
        compiler_params=_cparams("arbitrary"),
        name="hyena_filter",
    )(bands, w1p, b1p, w2p, b2p, w3p, frp, deltas)


FFT_COLS = 4096


def _split2_np(m):
    hi = jnp.asarray(m, F32).astype(BF16)
    lo = (jnp.asarray(m, F32) - hi.astype(F32)).astype(BF16)
    return hi, lo


def _three_pass_lhs(m):
    hi, lo = _split2_np(m)
    return jnp.concatenate([hi, hi, lo], axis=1)


def _three_pass_rhs(x):
    hi = x.astype(BF16)
    lo = (x - hi.astype(F32)).astype(BF16)
    return jnp.concatenate([hi, lo, hi], axis=0)


@functools.lru_cache(maxsize=None)
def _fft_plan(length):
    n = 2 * length
    n1, n2 = (128, n // 128) if n >= 4096 else (32, n // 32)
    k1 = n1 // 2 + 1
    k1p = -(-k1 // SUBLANE) * SUBLANE
    return n, n1, n2, k1, k1p


def _fft_consts(length):
    n, n1, n2, k1, k1p = _fft_plan(length)
    kk = np.arange(k1p)[:, None]
    live = (kk < k1).astype(np.float64)
    t1 = np.arange(n1)[None, :]
    ang = -2.0 * np.pi * kk * t1 / n1
    fa = np.concatenate([np.cos(ang) * live, np.sin(ang) * live], axis=0)
    t2 = np.arange(n2)[None, :]
    angw = -2.0 * np.pi * kk * t2 / n
    twr, twi = np.cos(angw) * live, np.sin(angw) * live
    a2 = -2.0 * np.pi * np.arange(n2)[:, None] * np.arange(n2)[None, :] / n2
    fr, fi = np.cos(a2), np.sin(a2)
    fwd = np.block([[fr, -fi], [fi, fr]])
    inv = np.block([[fr, fi], [-fi, fr]])
    wk = np.where((np.arange(k1p) == 0) | (np.arange(k1p) == n1 // 2), 1.0, 2.0) * (np.arange(k1p) < k1)
    th = 2.0 * np.pi * np.arange(n1 // 2)[:, None] * np.arange(k1p)[None, :] / n1
    g = np.concatenate([np.cos(th) * wk[None, :], -np.sin(th) * wk[None, :]], axis=1) / n
    return dict(fa_half=_three_pass_lhs(fa[:, :n1 // 2]), fa_full=_three_pass_lhs(fa),
                twr=jnp.asarray(twr[:, :, None], F32), twi=jnp.asarray(twi[:, :, None], F32),
                fwd=_three_pass_lhs(fwd), inv=_three_pass_lhs(inv), g=_three_pass_lhs(g))


def _fft_a_kernel(f_ref, x_ref, o_ref):
    o_ref[...] = jnp.dot(f_ref[...], _three_pass_rhs(x_ref[...]), preferred_element_type=F32)


def _fft_stage_a(fa, x2d):
    r, cols = x2d.shape
    m = fa.shape[0]
    tc = min(FFT_COLS, cols)
    return pl.pallas_call(
        _fft_a_kernel,
        grid=(cols // tc,),
        in_specs=[_full(fa.shape), pl.BlockSpec((r, tc), lambda j: (0, j))],
        out_specs=pl.BlockSpec((m, tc), lambda j: (0, j)),
        out_shape=jax.ShapeDtypeStruct((m, cols), F32),
        compiler_params=_cparams("arbitrary"),
        name="fft_stage_a",
    )(fa, x2d)


def _cmul(ar, ai, br, bi):
    return ar * br - ai * bi, ar * bi + ai * br


def _fft_c_fwd(f_ref, twr_ref, twi_ref, ar, ai):
    n2 = ar.shape[0]
    br, bi = _cmul(ar, ai, twr_ref[0], twi_ref[0])
    out = jnp.dot(f_ref[...], _three_pass_rhs(jnp.concatenate([br, bi], axis=0)), preferred_element_type=F32)
    return out[:n2], out[n2:]


def _fft_kf_kernel(f_ref, twr_ref, twi_ref, a_ref, o_ref):
    xr, xi = _fft_c_fwd(f_ref, twr_ref, twi_ref, a_ref[0, 0], a_ref[1, 0])
    o_ref[0, 0] = xr
    o_ref[1, 0] = xi


def _fft_mid_kernel(f_ref, fi_ref, twr_ref, twi_ref, a_ref, kf_ref, o_ref):
    n2 = a_ref.shape[2]
    xr, xi = _fft_c_fwd(f_ref, twr_ref, twi_ref, a_ref[0, 0], a_ref[1, 0])
    yr, yi = _cmul(xr, xi, kf_ref[0, 0], kf_ref[1, 0])
    out = jnp.dot(fi_ref[...], _three_pass_rhs(jnp.concatenate([yr, yi], axis=0)), preferred_element_type=F32)
    br, bi = _cmul(out[:n2], out[n2:], twr_ref[0], -twi_ref[0])
    o_ref[0, 0] = br
    o_ref[1, 0] = bi


def _fft_slab_specs(k1p, n2, c):
    slab = pl.BlockSpec((2, 1, n2, c), lambda k: (0, k, 0, 0))
    tw = pl.BlockSpec((1, n2, 1), lambda k: (k, 0, 0))
    return slab, tw


def _fft_ainv_kernel(g_ref, b_ref, nrm_ref, d_ref, u_ref, x0_ref, o_ref):
    y = jnp.dot(g_ref[...], _three_pass_rhs(b_ref[...]), preferred_element_type=F32)
    o_ref[...] = x0_ref[...] * (y / nrm_ref[...] + u_ref[...] * d_ref[...])


def hyena_long_conv(u, x0, k, nrm, d_bias):
    length, c = u.shape
    n, n1, n2, k1, k1p = _fft_plan(length)
    cs = _fft_consts(length)
    cols = n2 * c
    a_u = _fft_stage_a(cs["fa_half"], u.reshape(n1 // 2, cols)).reshape(2, k1p, n2, c)
    a_k = _fft_stage_a(cs["fa_full"], k.reshape(n1, cols)).reshape(2, k1p, n2, c)
    slab, tw = _fft_slab_specs(k1p, n2, c)
    blk = _full(cs["fwd"].shape)
    kf = pl.pallas_call(
        _fft_kf_kernel, grid=(k1p,), in_specs=[blk, tw, tw, slab], out_specs=slab,
        out_shape=jax.ShapeDtypeStruct((2, k1p, n2, c), F32),
        compiler_params=_cparams("arbitrary"), name="fft_filter_spectrum",
    )(cs["fwd"], cs["twr"], cs["twi"], a_k)
    b = pl.pallas_call(
        _fft_mid_kernel, grid=(k1p,), in_specs=[blk, blk, tw, tw, slab, slab], out_specs=slab,
        out_shape=jax.ShapeDtypeStruct((2, k1p, n2, c), F32),
        compiler_params=_cparams("arbitrary"), name="fft_mid",
    )(cs["fwd"], cs["inv"], cs["twr"], cs["twi"], a_u, kf)
    tc = min(FFT_COLS, cols)
    rep = tc // c
    nrm_t = jnp.tile(nrm.reshape(1, c), (1, rep))
    d_t = jnp.tile(d_bias.reshape(1, c).astype(F32), (1, rep))
    rows = n1 // 2
    col = pl.BlockSpec((rows, tc), lambda j: (0, j))
    y = pl.pallas_call(
        _fft_ainv_kernel,
        grid=(cols // tc,),
        in_specs=[_full(cs["g"].shape), pl.BlockSpec((2 * k1p, tc), lambda j: (0, j)), _full((1, tc)), _full((1, tc)),
                  col, col],
        out_specs=col,
        out_shape=jax.ShapeDtypeStruct((rows, cols), F32),
        compiler_params=_cparams("arbitrary"), name="fft_stage_a_inv",
    )(cs["g"], b.reshape(2 * k1p, cols), nrm_t, d_t, u.reshape(rows, cols), x0.reshape(rows, cols))
    return y.reshape(length, c)


ROUTE_PAD = LANE


def _rms(x, gain):
    return x * lax.rsqrt(jnp.mean(x * x, axis=-1, keepdims=True) + RMS_EPS) * gain


def _postmix_kernel(ys_ref, xbc_ref, z_ref, na_ref, hy_ref, x_ref, mod_ref, g_ref, dsk_ref, sg_ref,
                    wo_ref, wr_ref, br_ref, x1_ref, h2_ref, idx_ref, gw_ref):
    is_ctx = pl.program_id(0) == 0
    d = D_MODEL
    y = ys_ref[0] + ys_ref[1] + dsk_ref[...] * xbc_ref[...]
    z = z_ref[...]
    gated = y * _silu(z)
    gsz = SSD_WIDTH // SSD_GROUPS
    lane = lax.broadcasted_iota(jnp.int32, gated.shape, 1)
    g0 = lane < gsz
    sq = gated * gated
    m0 = jnp.sum(jnp.where(g0, sq, 0.0), axis=-1, keepdims=True) / gsz
    m1 = jnp.sum(jnp.where(g0, 0.0, sq), axis=-1, keepdims=True) / gsz
    y_ssd = gated * jnp.where(g0, lax.rsqrt(m0 + RMS_EPS), lax.rsqrt(m1 + RMS_EPS)) * sg_ref[...]
    mix = (jnp.dot(y_ssd.astype(BF16), wo_ref[0:SSD_WIDTH, :], preferred_element_type=F32)
           + jnp.dot(na_ref[...].astype(BF16), wo_ref[SSD_WIDTH:SSD_WIDTH + NA_WIDTH, :], preferred_element_type=F32)
           + jnp.dot(hy_ref[...].astype(BF16), wo_ref[SSD_WIDTH + NA_WIDTH:, :], preferred_element_type=F32))
    g1 = jnp.where(is_ctx, mod_ref[1:2, 2 * d:3 * d], mod_ref[0:1, 2 * d:3 * d])
    x1 = x_ref[...] + g1 * _rms(mix, g_ref[1:2, :])
    x1_ref[...] = x1
    h2f = _modulated_norm(x1, g_ref[2:3, :], mod_ref, 1, is_ctx)
    h2_ref[...] = h2f
    h2 = h2f.astype(BF16)
    logits = jnp.dot(h2, wr_ref[...], preferred_element_type=F32) + br_ref[...]
    ln = lax.broadcasted_iota(jnp.int32, logits.shape, 1)
    logits = jnp.where(ln < N_EXPERTS, logits, NEG_BIG)
    idx_out = jnp.zeros(logits.shape, jnp.int32)
    top = []
    for k in range(TOP_K):
        m = jnp.max(logits, axis=-1, keepdims=True)
        sel = jnp.min(jnp.where(logits == m, ln, ROUTE_PAD), axis=-1, keepdims=True)
        idx_out = jnp.where(ln == k, sel, idx_out)
        top.append(m)
        logits = jnp.where(ln == sel, NEG_BIG, logits)
    ex = [jnp.exp(v - top[0]) for v in top]
    den = ex[0] + ex[1] + ex[2] + ex[3]
    gw = jnp.zeros(logits.shape, F32)
    for k in range(TOP_K):
        gw = jnp.where(ln == k, ex[k] / den, gw)
    idx_ref[...] = idx_out
    gw_ref[...] = gw


def post_mixers(y_dirs, xbc_act, p_z, y_na, y_hy, xt, mod, gains, ssd_d, ssd_norm_g, w_out, w_router, b_router):
    t = xt.shape[0]
    dsk = jnp.repeat(ssd_d.astype(F32), SSD_HEAD_DIM).reshape(1, SSD_WIDTH)
    wr = jnp.pad(w_router, ((0, 0), (0, ROUTE_PAD - N_EXPERTS))).astype(BF16)
    br = jnp.pad(b_router.reshape(1, -1), ((0, 0), (0, ROUTE_PAD - N_EXPERTS)))

    def tile(w):
        return pl.BlockSpec((TM, w), lambda i: (i, 0))

    return pl.pallas_call(
        _postmix_kernel,
        grid=(t // TM,),
        in_specs=[pl.BlockSpec((2, TM, SSD_WIDTH), lambda i: (0, i, 0)), tile(SSD_WIDTH), tile(SSD_WIDTH),
                  tile(NA_WIDTH), tile(HY_WIDTH), tile(D_MODEL),
                  _full((SUBLANE, 6 * D_MODEL)), _full((4, D_MODEL)), _full((1, SSD_WIDTH)), _full((1, SSD_WIDTH)),
                  _full((D_MODEL, D_MODEL)), _full((D_MODEL, ROUTE_PAD)), _full((1, ROUTE_PAD))],
        out_specs=[tile(D_MODEL), tile(D_MODEL), tile(ROUTE_PAD), tile(ROUTE_PAD)],
        out_shape=[jax.ShapeDtypeStruct((t, D_MODEL), F32), jax.ShapeDtypeStruct((t, D_MODEL), F32),
                   jax.ShapeDtypeStruct((t, ROUTE_PAD), jnp.int32), jax.ShapeDtypeStruct((t, ROUTE_PAD), F32)],
        compiler_params=_cparams("arbitrary"),
        name="post_mixers",
    )(y_dirs, xbc_act, p_z, y_na, y_hy, xt, mod, gains, dsk, ssd_norm_g.reshape(1, -1), w_out.astype(BF16), wr, br)


MOE_BM = 256


def moe_plan(top_idx):
    nt = top_idx.shape[0]
    n = nt * TOP_K
    bm = MOE_BM
    nb = n // bm + N_EXPERTS
    i32 = jnp.int32
    e_flat = top_idx.reshape(n)
    order = jnp.argsort(e_flat, stable=True).astype(i32)
    experts = jnp.arange(N_EXPERTS, dtype=i32)
    counts = jnp.sum((e_flat[:, None] == experts[None, :]).astype(i32), axis=0)
    start = jnp.cumsum(counts) - counts
    nblk = (counts + bm - 1) // bm
    cum_blk = jnp.cumsum(nblk)
    n_used = cum_blk[-1]
    b = jnp.arange(nb, dtype=i32)
    active = b < n_used
    b_eff = jnp.minimum(b, n_used - 1)
    e_b = jnp.minimum(jnp.sum((cum_blk[None, :] <= b_eff[:, None]).astype(i32), axis=1), N_EXPERTS - 1)
    pick = (e_b[:, None] == experts[None, :]).astype(i32)
    j = b_eff - jnp.sum(pick * (cum_blk - nblk)[None, :], axis=1)
    pos0 = jnp.sum(pick * start[None, :], axis=1) + j * bm
    blen = jnp.where(active, jnp.clip(jnp.sum(pick * counts[None, :], axis=1) - j * bm, 0, bm), 0)
    first = jnp.concatenate([jnp.ones((1,), i32), (e_b[1:] != e_b[:-1]).astype(i32)])
    windows = jnp.concatenate([order, jnp.zeros((2 * bm,), i32)]).reshape(n // bm + 2, 1, bm)
    return (e_b.astype(i32), first, active.astype(i32), (pos0 // bm).astype(i32), (pos0 % bm).astype(i32),
            blen.astype(i32), windows)


MOE_UNROLL = 8


def _moe_kernel(eb_ref, first_ref, act_ref, win_ref, off_ref, len_ref,
                ca_ref, cb_ref, na_ref, nbk_ref, h_hbm, wgu_ref, bgu_ref, wd_ref, bd_ref,
                y_hbm, xbuf, obuf, wgu_bf, wd_bf, gsem, ssem, *, row0):
    b = pl.program_id(0)
    nb = pl.num_programs(0)
    bm = MOE_BM
    slot = b % 2
    n_tok = y_hbm.shape[1] - 2 * bm
    nxt = jnp.minimum(b + 1, nb - 1)

    def src_of(a_ref, b_ref, off, i):
        p = off + i
        return jnp.where(p < bm, a_ref[0, 0, jnp.minimum(p, bm - 1)], b_ref[0, 0, jnp.maximum(p - bm, 0)])

    def gather(a_ref, b_ref, off, s):
        def body(i, carry):
            tok = lax.shift_right_logical(src_of(a_ref, b_ref, off, i), 2) + row0
            pltpu.make_async_copy(h_hbm.at[pl.ds(tok, 1)], xbuf.at[s, pl.ds(i, 1)], gsem.at[s]).start()
            return carry
        lax.fori_loop(0, bm, body, 0, unroll=MOE_UNROLL)

    def wait_gather(s):
        pltpu.make_async_copy(h_hbm.at[pl.ds(0, bm)], xbuf.at[s], gsem.at[s]).wait()

    def scatter(s):
        off = off_ref[b]
        blen = len_ref[b]

        def body(i, carry):
            src = src_of(ca_ref, cb_ref, off, i)
            valid = i < blen
            k = jnp.where(valid, jnp.bitwise_and(src, TOP_K - 1), 0)
            t = jnp.where(valid, lax.shift_right_logical(src, 2), n_tok + s * bm + i)
            pltpu.make_async_copy(obuf.at[s, pl.ds(i, 1)], y_hbm.at[k, pl.ds(t, 1)], ssem.at[s]).start()
            return carry
        lax.fori_loop(0, bm, body, 0, unroll=MOE_UNROLL)

    def wait_scatter(s):
        pltpu.make_async_copy(obuf.at[s], y_hbm.at[0, pl.ds(0, bm)], ssem.at[s]).wait()

    @pl.when(b == 0)
    def _():
        obuf[0] = jnp.zeros((bm, D_MODEL), F32)
        for k in range(TOP_K):
            for stripe in range(2):
                cp = pltpu.make_async_copy(obuf.at[0], y_hbm.at[k, pl.ds(n_tok + stripe * bm, bm)], ssem.at[0])
                cp.start()
                cp.wait()

    @pl.when(jnp.logical_and(b == 0, act_ref[0] == 1))
    def _():
        gather(ca_ref, cb_ref, off_ref[0], 0)

    @pl.when(jnp.logical_and(b + 1 < nb, act_ref[nxt] == 1))
    def _():
        gather(na_ref, nbk_ref, off_ref[nxt], 1 - slot)

    @pl.when(first_ref[b] == 1)
    def _():
        wgu_bf[...] = wgu_ref[...].astype(BF16)
        wd_bf[...] = wd_ref[...].astype(BF16)

    @pl.when(jnp.logical_and(b >= 2, act_ref[jnp.maximum(b - 2, 0)] == 1))
    def _():
        wait_scatter(slot)

    @pl.when(act_ref[b] == 1)
    def _():
        wait_gather(slot)
        x = xbuf[slot].astype(BF16)
        gu = jnp.dot(x, wgu_bf[...], preferred_element_type=F32) + bgu_ref[...]
        gate = jnp.minimum(gu[:, :D_EXPERT], SWIGLU_LIMIT)
        up = jnp.clip(gu[:, D_EXPERT:], -SWIGLU_LIMIT, SWIGLU_LIMIT)
        a = gate * jax.nn.sigmoid(SWIGLU_ALPHA * gate) * (up + 1.0)
        obuf[slot] = jnp.dot(a.astype(BF16), wd_bf[...], preferred_element_type=F32) + bd_ref[...]
        scatter(slot)

    @pl.when(b == nb - 1)
    def _():
        @pl.when(jnp.logical_and(nb >= 2, act_ref[jnp.maximum(nb - 2, 0)] == 1))
        def _():
            wait_scatter(1 - slot)

        @pl.when(act_ref[nb - 1] == 1)
        def _():
            wait_scatter(slot)


def moe_experts(h2, row0, plan, w_gate_up, b_gate_up, w_down, b_down, layer):
    e_b, first, active, win, off, blen, windows = plan
    nb = e_b.shape[0]
    bm = MOE_BM
    n_tok = (windows.shape[0] - 2) * bm // TOP_K
    depth = w_gate_up.shape[0]

    def wspec(rows, cols):
        return pl.BlockSpec((None, None, rows, cols), lambda b, eb, fi, ac, wi, of, le: (layer, eb[b], 0, 0))

    def ispec(nxt, second):
        return pl.BlockSpec((1, 1, bm),
                            lambda b, eb, fi, ac, wi, of, le: (wi[jnp.minimum(b + nxt, nb - 1)] + second, 0, 0),
                            memory_space=pltpu.SMEM)

    grid_spec = pltpu.PrefetchScalarGridSpec(
        num_scalar_prefetch=6,
        grid=(nb,),
        in_specs=[ispec(0, 0), ispec(0, 1), ispec(1, 0), ispec(1, 1), pl.BlockSpec(memory_space=pl.ANY),
                  wspec(D_MODEL, 2 * D_EXPERT), wspec(1, 2 * D_EXPERT), wspec(D_EXPERT, D_MODEL), wspec(1, D_MODEL)],
        out_specs=pl.BlockSpec(memory_space=pl.ANY),
        scratch_shapes=[pltpu.VMEM((2, bm, D_MODEL), F32), pltpu.VMEM((2, bm, D_MODEL), F32),
                        pltpu.VMEM((D_MODEL, 2 * D_EXPERT), BF16), pltpu.VMEM((D_EXPERT, D_MODEL), BF16),
                        pltpu.SemaphoreType.DMA((2,)), pltpu.SemaphoreType.DMA((2,))],
    )
    return pl.pallas_call(
        functools.partial(_moe_kernel, row0=row0),
        grid_spec=grid_spec,
        out_shape=jax.ShapeDtypeStruct((TOP_K, n_tok + 2 * bm, D_MODEL), F32),
        compiler_params=_cparams("arbitrary"),
        name="moe_experts",
    )(e_b, first, active, win, off, blen, windows, windows, windows, windows, h2, w_gate_up,
      b_gate_up.reshape(depth, N_EXPERTS, 1, -1), w_down, b_down.reshape(depth, N_EXPERTS, 1, -1))


def _combine_kernel(y4_ref, gw_ref, x1_ref, mod_ref, g_ref, o_ref, *, off):
    is_ctx = pl.program_id(0) + off == 0
    d = D_MODEL
    gw = gw_ref[...]
    f = None
    for k in range(TOP_K):
        term = y4_ref[k] * gw[:, k:k + 1]
        f = term if f is None else f + term
    g2 = jnp.where(is_ctx, mod_ref[1:2, 5 * d:6 * d], mod_ref[0:1, 5 * d:6 * d])
    o_ref[...] = x1_ref[...] + g2 * _rms(f, g_ref[3:4, :])


def moe_combine(y4, gw, x1, mod, gains, row0):
    t = x1.shape[0]
    nt = (t - row0) // TM
    off = row0 // TM
    return pl.pallas_call(
        functools.partial(_combine_kernel, off=off),
        grid=(nt,),
        in_specs=[pl.BlockSpec((TOP_K, TM, D_MODEL), lambda i: (0, i, 0)),
                  pl.BlockSpec((TM, ROUTE_PAD), lambda i: (i + off, 0)),
                  pl.BlockSpec((TM, D_MODEL), lambda i: (i + off, 0)),
                  _full((SUBLANE, 6 * D_MODEL)), _full((4, D_MODEL))],
        out_specs=pl.BlockSpec((TM, D_MODEL), lambda i: (i, 0)),
        out_shape=jax.ShapeDtypeStruct((t - row0, D_MODEL), F32),
        compiler_params=_cparams("arbitrary"),
        name="moe_combine",
    )(y4, gw, x1, mod, gains)


STACKED = ("w_mod", "b_mod", "w_gate_up", "b_gate_up", "w_down", "b_down")


def trunk_layer(xt, cc, params, layer, last, n_ctx):
    t = xt.shape[0]
    lp = {name: (val if name in STACKED else val[layer]) for name, val in params.items()}
    gains = lp["norm_g"]
    mod = adaln_mod(cc, lp["w_mod"], lp["b_mod"], layer)
    p_z, p_xbc, p_dt, p_q, p_k, p_v, p_hy = in_projection(xt, mod, gains[0:1], pack_w_in(lp["w_in"]))
    xbc_act, dt, x0, u = pre_mixers(p_xbc, p_hy, p_dt, lp["ssd_conv_w"], lp["ssd_conv_b"], lp["hy_conv_w"],
                                    lp["hy_conv_b"], lp["ssd_dt_bias"])
    y_dirs = ssd_scan(xbc_act, dt, lp["ssd_a_log"], n_ctx)
    y_na = na_attention(p_q, p_k, p_v, na_bias_tiles(lp["na_rpb"]), n_ctx)
    f_args = (lp["hy_f_w1"], lp["hy_f_b1"], lp["hy_f_w2"], lp["hy_f_b2"], lp["hy_f_w3"], lp["hy_freq"])
    k_lat, nrm_lat = hyena_filter(t - n_ctx, *f_args)
    y_hy_lat = hyena_long_conv(u[n_ctx:], x0[n_ctx:], k_lat, nrm_lat, lp["hy_d"])
    if last:
        y_hy_ctx = jnp.zeros((n_ctx, HY_WIDTH), F32)
    else:
        k_ctx, nrm_ctx = hyena_filter(n_ctx, *f_args)
        y_hy_ctx = hyena_long_conv(u[:n_ctx], x0[:n_ctx], k_ctx, nrm_ctx, lp["hy_d"])
    y_hy = jnp.concatenate([y_hy_ctx, y_hy_lat], axis=0)
    x1, h2, idx, gw = post_mixers(y_dirs, xbc_act, p_z, y_na, y_hy, xt, mod, gains, lp["ssd_d"], lp["ssd_norm_g"],
                                  lp["w_out"], lp["w_router"], lp["b_router"])
    row0 = n_ctx if last else 0
    plan = moe_plan(idx[row0:, :TOP_K])
    y4 = moe_experts(h2, row0, plan, lp["w_gate_up"], lp["b_gate_up"], lp["w_down"], lp["b_down"], layer)
    return moe_combine(y4, gw, x1, mod, gains, row0)


def kernel(x, c, ctx, c_ctx, w_mod, b_mod, norm_g, w_in, ssd_conv_w, ssd_conv_b, ssd_a_log, ssd_dt_bias, ssd_d,
           ssd_norm_g, na_rpb, hy_conv_w, hy_conv_b, hy_f_w1, hy_f_b1, hy_f_w2, hy_f_b2, hy_f_w3, hy_freq, hy_d,
           w_out, w_router, b_router, w_gate_up, b_gate_up, w_down, b_down):
    params = dict(w_mod=w_mod, b_mod=b_mod, norm_g=norm_g, w_in=w_in, ssd_conv_w=ssd_conv_w, ssd_conv_b=ssd_conv_b,
                  ssd_a_log=ssd_a_log, ssd_dt_bias=ssd_dt_bias, ssd_d=ssd_d, ssd_norm_g=ssd_norm_g, na_rpb=na_rpb,
                  hy_conv_w=hy_conv_w, hy_conv_b=hy_conv_b, hy_f_w1=hy_f_w1, hy_f_b1=hy_f_b1, hy_f_w2=hy_f_w2,
                  hy_f_b2=hy_f_b2, hy_f_w3=hy_f_w3, hy_freq=hy_freq, hy_d=hy_d, w_out=w_out, w_router=w_router,
                  b_router=b_router, w_gate_up=w_gate_up, b_gate_up=b_gate_up, w_down=w_down, b_down=b_down)
    depth = w_mod.shape[0]
    batch, seq, d = x.shape
    assert batch == 1 and d == D_MODEL and seq % TM == 0 and ctx.shape[1] == TM
    n_ctx = ctx.shape[1]
    xt = jnp.concatenate([ctx[0], x[0]], axis=0)
    cc = jnp.concatenate([c[0:1], c_ctx.reshape(1, d), jnp.zeros((SUBLANE - 2, d), F32)], axis=0)
    for i in range(depth):
        xt = trunk_layer(xt, cc, params, i, i == depth - 1, n_ctx)
    return xt.reshape(1, seq, d)
```
